```python
import math
import jax, jax.numpy as jnp
from jax import lax
import numpy as np

D_MODEL = 1024
BATCH = 4
SEQ = 4096
DEPTH = 4
DEC_BATCH = 128
DEC_SEQ = 1
PAST_LEN = 8192
PAGE_SIZE = 128

GLA_HEADS = 4
GLA_DK = 32
GLA_DV = 64
GLA_GATE_RANK = 16
GLA_TAU = 16.0
GLA_CHUNK = 16
S5_GROUPS = 16
S5_GROUP_CH = 16
S5_STATE = 64
S5_WIDTH = S5_GROUPS * S5_GROUP_CH
S5_DT_MIN = 0.001
S5_DT_MAX = 0.1
MLA_HEADS = 8
MLA_Q_RANK = 256
MLA_KV_RANK = 128
MLA_NOPE = 64
MLA_ROPE = 32
MLA_DV = 64
ROPE_THETA = 10000.0
Q_BLOCK = 128
D_FF = 2816
NORM_EPS = 1e-6
POOL_EXTRA_DEN = 4
N_MOD = 9

GLA_QK_W = GLA_HEADS * GLA_DK
GLA_V_W = GLA_HEADS * GLA_DV
MLA_OUT_W = MLA_HEADS * MLA_DV
MIX_WIDTH = GLA_V_W + S5_WIDTH + MLA_OUT_W
IN_SPLITS = (GLA_QK_W, GLA_QK_W, GLA_V_W, GLA_GATE_RANK, GLA_V_W, S5_WIDTH, MLA_Q_RANK, MLA_KV_RANK, MLA_ROPE)
IN_WIDTH = sum(IN_SPLITS)
F32 = jnp.float32

kernel_name = 'hymba_gla_s5_mla_macaron_adaln_step'


def rmsnorm(x, g):
    xf = x.astype(F32)
    y = xf * lax.rsqrt(jnp.mean(xf * xf, axis=-1, keepdims=True) + NORM_EPS)
    return (y * g.astype(F32)).astype(x.dtype)


def head_rmsnorm(y, g, n_heads):
    shp = y.shape
    yh = y.astype(F32).reshape(shp[:-1] + (n_heads, shp[-1] // n_heads))
    yh = yh * lax.rsqrt(jnp.mean(yh * yh, axis=-1, keepdims=True) + NORM_EPS)
    return (yh.reshape(shp) * g.astype(F32)).astype(y.dtype)


def swiglu(h, w_gate, w_up, w_down):
    return (jax.nn.silu(h @ w_gate) * (h @ w_up)) @ w_down


def rope(x, pos):
    half = x.shape[-1] // 2
    freqs = ROPE_THETA ** (-jnp.arange(half, dtype=F32) / half)
    ang = pos[:, None] * freqs[None, :]
    cos = jnp.cos(ang)[None, :, None, :]
    sin = jnp.sin(ang)[None, :, None, :]
    xf = x.astype(F32)
    x1, x2 = xf[..., :half], xf[..., half:]
    return jnp.concatenate([x1 * cos - x2 * sin, x1 * sin + x2 * cos], axis=-1).astype(x.dtype)


def gla_recurrence(q, k, v, log_a, s0):
    B, L, H, DK = q.shape
    DV = v.shape[-1]
    c = min(GLA_CHUNK, L)
    n = -(-L // c)
    pad = n * c - L
    padw = ((0, 0), (0, pad), (0, 0), (0, 0))
    q, k, v, log_a = [jnp.pad(t.astype(F32), padw).reshape(B, n, c, H, t.shape[-1]) for t in (q, k, v, log_a)]
    b = jnp.cumsum(log_a, axis=2)
    tri = jnp.tril(jnp.ones((c, c), bool))[:, :, None, None]
    diff = b[:, :, :, None] - b[:, :, None, :]
    decay = jnp.exp(jnp.where(tri, diff, -jnp.inf))
    attn = jnp.einsum('bnthd,bnshd,bntshd->bnhts', q, k, decay)
    o_intra = jnp.einsum('bnhts,bnshe->bnthe', attn, v)
    b_last = b[:, :, -1]
    k_dec = k * jnp.exp(b_last[:, :, None] - b)
    u = jnp.einsum('bnshd,bnshe->bnhde', k_dec, v)

    def step(s, inp):
        g, du = inp
        return jnp.exp(g)[..., None] * s + du, s

    s_fin, s_start = lax.scan(step, s0.astype(F32), (jnp.moveaxis(b_last, 1, 0), jnp.moveaxis(u, 1, 0)))
    s_start = jnp.moveaxis(s_start, 0, 1)
    o_inter = jnp.einsum('bnthd,bnhde->bnthe', q * jnp.exp(b), s_start)
    o = (o_intra + o_inter).reshape(B, n * c, H, DV)[:, :L]
    return o, s_fin


def s5_scan(u, h0_re, h0_im, a_re, a_im, log_dt, b_re, b_im, c_re, c_im, d):
    B, L, _ = u.shape
    uf = u.astype(F32)
    ug = uf.reshape(B, L, S5_GROUPS, S5_GROUP_CH)
    a_re = a_re.astype(F32)
    a_im = a_im.astype(F32)
    dt = jnp.exp(log_dt.astype(F32))[:, None]
    mag = jnp.exp(a_re * dt)
    lr, li = mag * jnp.cos(a_im * dt), mag * jnp.sin(a_im * dt)
    den = a_re * a_re + a_im * a_im
    zr = ((lr - 1.0) * a_re + li * a_im) / den
    zi = (li * a_re - (lr - 1.0) * a_im) / den
    br, bi = b_re.astype(F32), b_im.astype(F32)
    bb_re = zr[..., None] * br - zi[..., None] * bi
    bb_im = zr[..., None] * bi + zi[..., None] * br
    x_re = jnp.einsum('blgj,gnj->blgn', ug, bb_re)
    x_im = jnp.einsum('blgj,gnj->blgn', ug, bb_im)
    h0r, h0i = h0_re.astype(F32), h0_im.astype(F32)
    x_re = x_re.at[:, 0].add(lr * h0r - li * h0i)
    x_im = x_im.at[:, 0].add(lr * h0i + li * h0r)
    ar = jnp.broadcast_to(lr, x_re.shape)
    ai = jnp.broadcast_to(li, x_re.shape)

    def combine(e1, e2):
        ar1, ai1, br1, bi1 = e1
        ar2, ai2, br2, bi2 = e2
        return (ar1 * ar2 - ai1 * ai2, ar1 * ai2 + ai1 * ar2,
                ar2 * br1 - ai2 * bi1 + br2, ar2 * bi1 + ai2 * br1 + bi2)

    _, _, h_re, h_im = lax.associative_scan(combine, (ar, ai, x_re, x_im), axis=1)
    y = (jnp.einsum('blgn,gjn->blgj', h_re, c_re.astype(F32))
         - jnp.einsum('blgn,gjn->blgj', h_im, c_im.astype(F32)))
    y = y.reshape(B, L, S5_WIDTH) + d.astype(F32) * uf
    return y, h_re[:, -1], h_im[:, -1]


def mla_attend(q_lat, q_rope, q_pos, k_lat, k_rope, k_pos):
    B, L, H, R = q_lat.shape
    qb = min(Q_BLOCK, L)
    nb = -(-L // qb)
    pad = nb * qb - L
    q_lat = jnp.pad(q_lat, ((0, 0), (0, pad), (0, 0), (0, 0)))
    q_rope = jnp.pad(q_rope, ((0, 0), (0, pad), (0, 0), (0, 0)))
    q_pos = jnp.pad(q_pos, (0, pad), mode='edge')
    scale = (MLA_NOPE + MLA_ROPE) ** -0.5

    def to_blocks(t):
        return jnp.moveaxis(t.reshape((B, nb, qb) + t.shape[2:]), 1, 0)

    def block(args):
        ql, qr, qp = args
        s = (jnp.einsum('bqhr,bkr->bhqk', ql, k_lat)
             + jnp.einsum('bqhd,bkd->bhqk', qr, k_rope)).astype(F32) * scale
        s = jnp.where(k_pos[None, :] <= qp[:, None], s, -jnp.inf)
        p = jax.nn.softmax(s, axis=-1).astype(k_lat.dtype)
        return jnp.einsum('bhqk,bkr->bqhr', p, k_lat)

    ctx = lax.map(block, (to_blocks(q_lat), to_blocks(q_rope), q_pos.reshape(nb, qb)))
    return jnp.moveaxis(ctx, 0, 1).reshape(B, nb * qb, H, R)[:, :L]


def mla_branch(c_q, c_kv, k_r, past_lat, past_rope, q_norm, kv_norm, w_uq, w_uk, w_uv):
    B, L, _ = c_q.shape
    lp = past_lat.shape[1]
    pos_i = lp + jnp.arange(L)
    pos_f = pos_i.astype(F32)
    q = (rmsnorm(c_q, q_norm) @ w_uq).reshape(B, L, MLA_HEADS, MLA_NOPE + MLA_ROPE)
    q_nope = q[..., :MLA_NOPE]
    q_rope = rope(q[..., MLA_NOPE:], pos_f)
    ckv = rmsnorm(c_kv, kv_norm)
    krope = rope(k_r[:, :, None, :], pos_f)[:, :, 0]
    q_lat = jnp.einsum('blhd,rhd->blhr', q_nope, w_uk)
    keys_lat = jnp.concatenate([past_lat.astype(ckv.dtype), ckv], axis=1)
    keys_rope = jnp.concatenate([past_rope.astype(krope.dtype), krope], axis=1)
    ctx = mla_attend(q_lat, q_rope, pos_i, keys_lat, keys_rope, jnp.arange(lp + L))
    o = jnp.einsum('blhr,rhe->blhe', ctx, w_uv).reshape(B, L, MLA_OUT_W)
    return o, ckv, krope


def trunk_layer(x, c, past_lat, past_rope, gla_s0, s5_h0_re, s5_h0_im, lp):
    B, L, _ = x.shape
    dt = x.dtype
    mod = jax.nn.silu(c.astype(F32)) @ lp['ada_w'].astype(F32) + lp['ada_b'].astype(F32)
    sh1, sc1, g1, sh2, sc2, g2, sh3, sc3, g3 = [m[:, None, :] for m in jnp.split(mod, N_MOD, axis=-1)]

    def modulate(h, sh, sc):
        return (h.astype(F32) * (1.0 + sc) + sh).astype(dt)

    h = modulate(rmsnorm(x, lp['norm_ffn1']), sh1, sc1)
    x = x + (0.5 * g1 * swiglu(h, lp['ffn1_w_gate'], lp['ffn1_w_up'], lp['ffn1_w_down'])).astype(dt)

    h = modulate(rmsnorm(x, lp['norm_mix']), sh2, sc2)
    z = h @ lp['w_in']
    gq, gk, gv, ga, gg, su, mcq, mckv, mkr = jnp.split(z, np.cumsum(IN_SPLITS)[:-1].tolist(), axis=-1)

    q = gq.reshape(B, L, GLA_HEADS, GLA_DK) * (GLA_DK ** -0.5)
    k = gk.reshape(B, L, GLA_HEADS, GLA_DK)
    v = gv.reshape(B, L, GLA_HEADS, GLA_DV)
    log_a = jax.nn.log_sigmoid((ga @ lp['gla_w_a2'] + lp['gla_b_a']).astype(F32)) / GLA_TAU
    o, s_gla = gla_recurrence(q, k, v, log_a.reshape(B, L, GLA_HEADS, GLA_DK), gla_s0)
    o_gla = head_rmsnorm(o.reshape(B, L, GLA_V_W).astype(dt), lp['gla_o_norm'], GLA_HEADS) * jax.nn.silu(gg)

    y, s5_re, s5_im = s5_scan(su, s5_h0_re, s5_h0_im, lp['s5_a_re'], lp['s5_a_im'], lp['s5_log_dt'],
                              lp['s5_b_re'], lp['s5_b_im'], lp['s5_c_re'], lp['s5_c_im'], lp['s5_d'])
    yg = jax.nn.gelu(y)
    o_s5 = yg * jax.nn.sigmoid(yg @ lp['s5_glu_w'].astype(F32) + lp['s5_glu_b'].astype(F32))
    o_s5 = head_rmsnorm(o_s5.astype(dt), lp['s5_o_norm'], S5_GROUPS)

    o_mla, ckv, krope = mla_branch(mcq, mckv, mkr, past_lat, past_rope, lp['mla_q_norm'], lp['mla_kv_norm'],
                                   lp['mla_w_uq'], lp['mla_w_uk'], lp['mla_w_uv'])
    o_mla = head_rmsnorm(o_mla, lp['mla_o_norm'], MLA_HEADS)

    mix = jnp.concatenate([o_gla, o_s5, o_mla], axis=-1) @ lp['w_out']
    x = x + (g2 * mix).astype(dt)

    h = modulate(rmsnorm(x, lp['norm_ffn2']), sh3, sc3)
    x = x + (0.5 * g3 * swiglu(h, lp['ffn2_w_gate'], lp['ffn2_w_up'], lp['ffn2_w_down'])).astype(dt)
    return x, (s_gla, s5_re, s5_im, ckv, krope)


def setup_inputs(seed: int = 0) -> dict:
    key = jax.random.key(seed)
    keys = jax.random.split(key, 64)
    ks = iter([keys[i] for i in range(64)])

    def nrm(shape, scale=1.0):
        return jax.random.normal(next(ks), shape, F32) * scale

    def gain(shape):
        return 1.0 + nrm(shape, 0.05)

    n_pages = PAST_LEN // PAGE_SIZE
    n_used = DEC_BATCH * n_pages
    n_pool = n_used + n_used // POOL_EXTRA_DEN
    L, D, G, N, J = DEPTH, D_MODEL, S5_GROUPS, S5_STATE, S5_GROUP_CH
    return {
        'x_prompt': nrm((BATCH, SEQ, D)),
        'x_sample': nrm((DEC_BATCH, DEC_SEQ, D)),
        'state_gla': nrm((L, DEC_BATCH, GLA_HEADS, GLA_DK, GLA_DV), 0.5),
        'state_s5_re': nrm((L, DEC_BATCH, G, N), 0.1),
        'state_s5_im': nrm((L, DEC_BATCH, G, N), 0.1),
        'cache_kv_latent': nrm((L, n_pool, PAGE_SIZE, MLA_KV_RANK)),
        'cache_k_rope': nrm((L, n_pool, PAGE_SIZE, MLA_ROPE)),
        'page_table': jax.random.permutation(next(ks), n_pool)[:n_used].reshape(DEC_BATCH, n_pages).astype(jnp.int32),
        'c_prompt': nrm((BATCH, D)),
        'c_sample': nrm((DEC_BATCH, D)),
        'ada_w': nrm((L, D, N_MOD * D), 0.5 * D ** -0.5),
        'ada_b': nrm((L, N_MOD * D), 0.01),
        'norm_ffn1': gain((L, D)),
        'ffn1_w_gate': nrm((L, D, D_FF), D ** -0.5),
        'ffn1_w_up': nrm((L, D, D_FF), D ** -0.5),
        'ffn1_w_down': nrm((L, D_FF, D), D_FF ** -0.5),
        'norm_mix': gain((L, D)),
        'w_in': nrm((L, D, IN_WIDTH), D ** -0.5),
        'gla_w_a2': nrm((L, GLA_GATE_RANK, GLA_QK_W), GLA_GATE_RANK ** -0.5),
        'gla_b_a': nrm((L, GLA_QK_W), 0.01),
        'gla_o_norm': gain((L, GLA_V_W)),
        's5_a_re': -0.5 + nrm((L, G, N), 0.01),
        's5_a_im': math.pi * jnp.arange(N, dtype=F32) + nrm((L, G, N), 0.01),
        's5_log_dt': jax.random.uniform(next(ks), (L, G), F32, math.log(S5_DT_MIN), math.log(S5_DT_MAX)),
        's5_b_re': nrm((L, G, N, J), (2 * J) ** -0.5),
        's5_b_im': nrm((L, G, N, J), (2 * J) ** -0.5),
        's5_c_re': nrm((L, G, J, N), (2 * N) ** -0.5),
        's5_c_im': nrm((L, G, J, N), (2 * N) ** -0.5),
        's5_d': nrm((L, S5_WIDTH)),
        's5_glu_w': nrm((L, S5_WIDTH, S5_WIDTH), S5_WIDTH ** -0.5),
        's5_glu_b': nrm((L, S5_WIDTH), 0.01),
        's5_o_norm': gain((L, S5_WIDTH)),
        'mla_q_norm': gain((L, MLA_Q_RANK)),
        'mla_kv_norm': gain((L, MLA_KV_RANK)),
        'mla_w_uq': nrm((L, MLA_Q_RANK, MLA_HEADS * (MLA_NOPE + MLA_ROPE)), MLA_Q_RANK ** -0.5),
        'mla_w_uk': nrm((L, MLA_KV_RANK, MLA_HEADS, MLA_NOPE), MLA_KV_RANK ** -0.5),
        'mla_w_uv': nrm((L, MLA_KV_RANK, MLA_HEADS, MLA_DV), MLA_KV_RANK ** -0.5),
        'mla_o_norm': gain((L, MLA_OUT_W)),
        'w_out': nrm((L, MIX_WIDTH, D), MIX_WIDTH ** -0.5),
        'norm_ffn2': gain((L, D)),
        'ffn2_w_gate': nrm((L, D, D_FF), D ** -0.5),
        'ffn2_w_up': nrm((L, D, D_FF), D ** -0.5),
        'ffn2_w_down': nrm((L, D_FF, D), D_FF ** -0.5),
        'final_norm': gain((D,)),
    }


def reference(x_prompt, x_sample, state_gla, state_s5_re, state_s5_im, cache_kv_latent, cache_k_rope, page_table,
              c_prompt, c_sample, ada_w, ada_b, norm_ffn1, ffn1_w_gate, ffn1_w_up, ffn1_w_down, norm_mix, w_in,
              gla_w_a2, gla_b_a, gla_o_norm, s5_a_re, s5_a_im, s5_log_dt, s5_b_re, s5_b_im, s5_c_re, s5_c_im, s5_d,
              s5_glu_w, s5_glu_b, s5_o_norm, mla_q_norm, mla_kv_norm, mla_w_uq, mla_w_uk, mla_w_uv, mla_o_norm,
              w_out, norm_ffn2, ffn2_w_gate, ffn2_w_up, ffn2_w_down, final_norm):
    bp = x_prompt.shape[0]
    bs = x_sample.shape[0]
    xp, xs = x_prompt, x_sample
    empty_lat = jnp.zeros((bp, 0, MLA_KV_RANK), x_prompt.dtype)
    empty_rope = jnp.zeros((bp, 0, MLA_ROPE), x_prompt.dtype)
    zero_gla = jnp.zeros((bp, GLA_HEADS, GLA_DK, GLA_DV), F32)
    zero_s5 = jnp.zeros((bp, S5_GROUPS, S5_STATE), F32)
    outs_p, outs_s = [], []
    for l in range(DEPTH):
        lp = {
            'ada_w': ada_w[l], 'ada_b': ada_b[l],
            'norm_ffn1': norm_ffn1[l], 'ffn1_w_gate': ffn1_w_gate[l], 'ffn1_w_up': ffn1_w_up[l], 'ffn1_w_down': ffn1_w_down[l],
            'norm_mix': norm_mix[l], 'w_in': w_in[l],
            'gla_w_a2': gla_w_a2[l], 'gla_b_a': gla_b_a[l], 'gla_o_norm': gla_o_norm[l],
            's5_a_re': s5_a_re[l], 's5_a_im': s5_a_im[l], 's5_log_dt': s5_log_dt[l],
            's5_b_re': s5_b_re[l], 's5_b_im': s5_b_im[l], 's5_c_re': s5_c_re[l], 's5_c_im': s5_c_im[l], 's5_d': s5_d[l],
            's5_glu_w': s5_glu_w[l], 's5_glu_b': s5_glu_b[l], 's5_o_norm': s5_o_norm[l],
            'mla_q_norm': mla_q_norm[l], 'mla_kv_norm': mla_kv_norm[l], 'mla_w_uq': mla_w_uq[l],
            'mla_w_uk': mla_w_uk[l], 'mla_w_uv': mla_w_uv[l], 'mla_o_norm': mla_o_norm[l],
            'w_out': w_out[l],
            'norm_ffn2': norm_ffn2[l], 'ffn2_w_gate': ffn2_w_gate[l], 'ffn2_w_up': ffn2_w_up[l], 'ffn2_w_down': ffn2_w_down[l],
        }
        xp, st_p = trunk_layer(xp, c_prompt, empty_lat, empty_rope, zero_gla, zero_s5, zero_s5, lp)
        past_lat = cache_kv_latent[l][page_table].reshape(bs, -1, MLA_KV_RANK)
        past_rope = cache_k_rope[l][page_table].reshape(bs, -1, MLA_ROPE)
        xs, st_s = trunk_layer(xs, c_sample, past_lat, past_rope, state_gla[l], state_s5_re[l], state_s5_im[l], lp)
        outs_p.append(st_p)
        outs_s.append(st_s)
    y_prompt = rmsnorm(xp, final_norm)
    y_sample = rmsnorm(xs, final_norm)
    new_state_gla_prompt = jnp.stack([o[0] for o in outs_p], axis=0)
    new_state_gla_sample = jnp.stack([o[0] for o in outs_s], axis=0)
    new_state_s5_re_prompt = jnp.stack([o[1] for o in outs_p], axis=0)
    new_state_s5_im_prompt = jnp.stack([o[2] for o in outs_p], axis=0)
    new_state_s5_re_sample = jnp.stack([o[1] for o in outs_s], axis=0)
    new_state_s5_im_sample = jnp.stack([o[2] for o in outs_s], axis=0)
    new_kv_latent_prompt = jnp.stack([o[3] for o in outs_p], axis=0)
    new_k_rope_prompt = jnp.stack([o[4] for o in outs_p], axis=0)
    new_kv_latent_sample = jnp.stack([o[3] for o in outs_s], axis=0)
    new_k_rope_sample = jnp.stack([o[4] for o in outs_s], axis=0)
    return (y_prompt, y_sample, new_state_gla_prompt, new_state_gla_sample,
            new_state_s5_re_prompt, new_state_s5_im_prompt, new_state_s5_re_sample, new_state_s5_im_sample,
            new_kv_latent_prompt, new_k_rope_prompt, new_kv_latent_sample, new_k_rope_sample)
```

```python
import functools
import math

import numpy as np
import jax
import jax.numpy as jnp
from jax import lax
from jax.experimental import pallas as pl
from jax.experimental.pallas import tpu as pltpu

F32 = jnp.float32
BF16 = jnp.bfloat16

GLA_H, GLA_DK, GLA_DV = 4, 32, 64
GLA_RANK = 16
GLA_TAU = 16.0
GLA_CHUNK = 16
S5_G, S5_J, S5_N = 16, 16, 64
MLA_H, MLA_QR, MLA_R, MLA_NOPE, MLA_ROPE, MLA_DV = 8, 256, 128, 64, 32, 64
ROPE_THETA = 10000.0
EPS = 1e-6
N_MOD = 9

GLA_QK = GLA_H * GLA_DK
GLA_V = GLA_H * GLA_DV
S5_W = S5_G * S5_J
S5_S = S5_G * S5_N
MLA_O = MLA_H * MLA_DV
LANES = 128
Z_WIDTH = 13 * LANES

VMEM_LIMIT = 56 * 1024 * 1024


def _mm(a, b):
    return jnp.dot(a, b, preferred_element_type=F32)


def _mm_nt(a, b):
    return lax.dot_general(a, b, (((1,), (1,)), ((), ())), preferred_element_type=F32)


def _rms(x, g):
    return x * lax.rsqrt(jnp.mean(x * x, axis=-1, keepdims=True) + EPS) * g


def _group_rms(o, ones_bd, group, g):
    ms = _mm((o * o).astype(BF16), ones_bd) * (1.0 / group)
    return o * lax.rsqrt(ms + EPS) * g


def _split_bf16(x):
    hi = x.astype(BF16)
    lo = (x - hi.astype(F32)).astype(BF16)
    return hi, lo


def _cparams(sem):
    return pltpu.CompilerParams(dimension_semantics=sem, vmem_limit_bytes=VMEM_LIMIT)


def _ada_body(c_ref, w_ref, b_ref, o_ref):
    s = jax.nn.silu(c_ref[...]).astype(BF16)
    o_ref[...] = _mm(s, w_ref[...].astype(BF16)) + b_ref[...]


def _ada_call(c_all, ada_w, ada_b):
    depth, d, nd = ada_w.shape
    mc = c_all.shape[0]
    tn = 9 * LANES
    return pl.pallas_call(
        _ada_body,
        out_shape=jax.ShapeDtypeStruct((depth, mc, nd), F32),
        grid=(depth, nd // tn),
        in_specs=[
            pl.BlockSpec((mc, d), lambda l, j: (0, 0)),
            pl.BlockSpec((None, d, tn), lambda l, j: (l, 0, j)),
            pl.BlockSpec((None, 1, tn), lambda l, j: (l, 0, j)),
        ],
        out_specs=pl.BlockSpec((None, mc, tn), lambda l, j: (l, 0, j)),
        compiler_params=_cparams(("arbitrary", "arbitrary")),
        name="ada_mod",
    )(c_all, ada_w, ada_b.reshape(depth, 1, nd))


class _Rows:
    def __init__(self, n_rows, tm, layer, mod_arr, per_row_mod, tiles_per_seq):
        self.n_rows, self.tm, self.layer = n_rows, tm, layer
        self.mod_arr, self.per_row_mod, self.tps = mod_arr, per_row_mod, tiles_per_seq
        self.grid = (n_rows // tm,)

    def rows(self, width):
        return pl.BlockSpec((self.tm, width), lambda i: (i, 0))

    def mod(self, j, d):
        l, tps = self.layer, self.tps
        if self.per_row_mod:
            return pl.BlockSpec((None, self.tm, d), lambda i: (l, i, j))
        return pl.BlockSpec((None, None, 1, d), lambda i: (l, i // tps, 0, j))

    def layer_w(self, shape):
        l = self.layer
        nd = len(shape)
        return pl.BlockSpec((None,) + tuple(shape), lambda i: (l,) + (0,) * nd)

    def pos(self, width):
        tps = self.tps
        return pl.BlockSpec((self.tm, width), lambda i: (i % tps, 0))


def _const(shape):
    nd = len(shape)
    return pl.BlockSpec(tuple(shape), lambda *_: (0,) * nd)


def _ffn_body(*refs, has_mix, final, n_chunks):
    it = iter(refs)
    x_ref = next(it)
    if has_mix:
        og_ref, os_ref, om_ref, g2_ref, wo_ref = [next(it) for _ in range(5)]
    sh_ref, sc_ref, g_ref, nw_ref, wg_ref, wu_ref, wd_ref = [next(it) for _ in range(7)]
    fn_ref = next(it) if final else None
    o_ref = next(it)

    x = x_ref[...]
    if has_mix:
        mix = (_mm(og_ref[...].astype(BF16), wo_ref[0:GLA_V, :])
               + _mm(os_ref[...].astype(BF16), wo_ref[GLA_V:GLA_V + S5_W, :])
               + _mm(om_ref[...].astype(BF16), wo_ref[GLA_V + S5_W:, :]))
        x = x + g2_ref[...] * mix
    h = (_rms(x, nw_ref[...]) * (1.0 + sc_ref[...]) + sh_ref[...]).astype(BF16)
    fc = wg_ref.shape[1] // n_chunks
    acc = None
    for c in range(n_chunks):
        gate = _mm(h, wg_ref[:, c * fc:(c + 1) * fc])
        up = _mm(h, wu_ref[:, c * fc:(c + 1) * fc])
        a = (jax.nn.silu(gate) * up).astype(BF16)
        part = _mm(a, wd_ref[c * fc:(c + 1) * fc, :])
        acc = part if acc is None else acc + part
    y = x + (0.5 * g_ref[...]) * acc
    if final:
        y = _rms(y, fn_ref[...])
    o_ref[...] = y


def _ffn_call(rows, x, mod_base, nw, wg, wu, wd, mix=None, final_norm=None):
    d = x.shape[1]
    f = wg.shape[2]
    ins, specs = [x], [rows.rows(d)]
    if mix is not None:
        og, os_, om, wo = mix
        ins += [og, os_, om, rows.mod_arr, wo]
        specs += [rows.rows(GLA_V), rows.rows(S5_W), rows.rows(MLA_O), rows.mod(5, d),
                  rows.layer_w(wo.shape[1:])]
    ins += [rows.mod_arr, rows.mod_arr, rows.mod_arr, nw, wg, wu, wd]
    specs += [rows.mod(mod_base, d), rows.mod(mod_base + 1, d), rows.mod(mod_base + 2, d),
              rows.layer_w((1, d)), rows.layer_w((d, f)), rows.layer_w((f, d))]
    specs.insert(len(specs) - 1, rows.layer_w((d, f)))
    if final_norm is not None:
        ins.append(final_norm)
        specs.append(_const((1, d)))
    body = functools.partial(_ffn_body, has_mix=mix is not None, final=final_norm is not None, n_chunks=2)
    return pl.pallas_call(
        body,
        out_shape=jax.ShapeDtypeStruct(x.shape, F32),
        grid=rows.grid,
        in_specs=specs,
        out_specs=rows.rows(d),
        compiler_params=_cparams(("arbitrary",)),
        name="ffn_mix" if mix is not None else "ffn",
    )(*ins)


def _inproj_body(x_ref, sh_ref, sc_ref, nw_ref, win_ref, wa_ref, ba_ref, qn_ref, wn_ref, wr_ref, wrr_ref,
                 wk_ref, kvn_ref, cos_ref, sin_ref,
                 qkl_ref, v_ref, sg_ref, su_ref, qcat_ref, kcat_ref, ckv_ref, kr_ref):
    x = x_ref[...]
    h = (_rms(x, nw_ref[...]) * (1.0 + sc_ref[...]) + sh_ref[...]).astype(BF16)
    z = _mm(h, win_ref[...])
    L = LANES
    qkl_ref[:, 0:L] = z[:, 0:L] * (GLA_DK ** -0.5)
    qkl_ref[:, L:2 * L] = z[:, L:2 * L]
    g11 = z[:, 11 * L:12 * L]
    gate = _mm(g11.astype(BF16), wa_ref[...]) + ba_ref[...]
    qkl_ref[:, 2 * L:3 * L] = jax.nn.log_sigmoid(gate) * (1.0 / GLA_TAU)
    v_ref[...] = z[:, 2 * L:4 * L]
    sg_ref[...] = jax.nn.silu(z[:, 4 * L:6 * L])
    su_ref[...] = z[:, 6 * L:8 * L]
    cos = cos_ref[...]
    sin = sin_ref[...]
    ckv = _rms(z[:, 10 * L:11 * L], kvn_ref[...])
    kr = g11 * cos + z[:, 12 * L:13 * L] * sin
    ckv_ref[...] = ckv
    kr_ref[...] = kr
    kcat_ref[:, 0:L] = ckv.astype(kcat_ref.dtype)
    kcat_ref[:, L:2 * L] = kr.astype(kcat_ref.dtype)
    cq = _rms(z[:, 8 * L:10 * L], qn_ref[...]).astype(BF16)
    q_nope = _mm(cq, wn_ref[...]).astype(BF16)
    r = _mm(cq, wr_ref[...])
    rr = _mm(cq, wrr_ref[...])
    scale = (MLA_NOPE + MLA_ROPE) ** -0.5
    for p in range(MLA_H // 2):
        ql = _mm(q_nope[:, p * L:(p + 1) * L], wk_ref[p])
        for e in range(2):
            hd = 2 * p + e
            qcat_ref[:, 2 * hd * L:(2 * hd + 1) * L] = (ql[:, e * L:(e + 1) * L] * scale).astype(qcat_ref.dtype)
            ro = r[:, hd * L:(hd + 1) * L] * cos + rr[:, hd * L:(hd + 1) * L] * sin
            qcat_ref[:, (2 * hd + 1) * L:(2 * hd + 2) * L] = (ro * scale).astype(qcat_ref.dtype)


def _inproj_call(rows, x, nw, pp, cos_tab, sin_tab, q_dtype):
    d = x.shape[1]
    n = x.shape[0]
    L = LANES
    ins = [x, rows.mod_arr, rows.mod_arr, nw, pp["w_in"], pp["w_a2"], pp["b_a"], pp["q_norm"], pp["w_n"],
           pp["w_r"], pp["w_rr"], pp["w_k"], pp["kv_norm"], cos_tab, sin_tab]
    specs = [rows.rows(d), rows.mod(3, d), rows.mod(4, d), rows.layer_w((1, d)),
             rows.layer_w((d, Z_WIDTH)), rows.layer_w((L, L)), rows.layer_w((1, L)),
             rows.layer_w((1, MLA_QR)), rows.layer_w((MLA_QR, MLA_H * MLA_NOPE)),
             rows.layer_w((MLA_QR, MLA_H * L)), rows.layer_w((MLA_QR, MLA_H * L)),
             rows.layer_w((MLA_H // 2, L, 2 * L)), rows.layer_w((1, L)),
             rows.pos(L), rows.pos(L)]
    widths = [(3 * L, F32), (GLA_V, F32), (GLA_V, F32), (S5_W, F32), (MLA_H * 2 * L, q_dtype),
              (2 * L, q_dtype), (L, F32), (L, F32)]
    return pl.pallas_call(
        _inproj_body,
        out_shape=[jax.ShapeDtypeStruct((n, w), dt) for w, dt in widths],
        grid=rows.grid,
        in_specs=specs,
        out_specs=[rows.rows(w) for w, _ in widths],
        compiler_params=_cparams(("arbitrary",)),
        name="in_proj",
    )(*ins)


def _gla_consts():
    t = np.arange(LANES)
    tri = ((t[:, None] // GLA_CHUNK == t[None, :] // GLA_CHUNK) & (t[None, :] <= t[:, None]))
    hd = np.arange(GLA_QK) // GLA_DK
    he = np.arange(GLA_V) // GLA_DV
    same = (hd[:, None] == he[None, :])
    return (jnp.asarray(tri, BF16), jnp.asarray(same, BF16), jnp.asarray(same, F32))


def _block_ones(width, group):
    g = np.arange(width) // group
    return jnp.asarray(g[:, None] == g[None, :], BF16)


def _gla_seq_body(qkl_ref, v_ref, sg_ref, gn_ref, tri_ref, ee_ref, bdm_ref, hn_ref, o_ref, sfin_ref, s_scr):
    R = LANES
    nc = R // GLA_CHUNK

    @pl.when(pl.program_id(1) == 0)
    def _():
        s_scr[...] = jnp.zeros_like(s_scr)

    q = qkl_ref[:, 0:R]
    k = qkl_ref[:, R:2 * R]
    la = qkl_ref[:, 2 * R:3 * R]
    v = v_ref[...]
    la_hi, la_lo = _split_bf16(la)
    tri = tri_ref[...]
    b = _mm(tri, la_hi) + _mm(tri, la_lo)
    b3 = b.reshape(nc, GLA_CHUNK, R)
    q3 = q.reshape(nc, GLA_CHUNK, R)
    k3 = k.reshape(nc, GLA_CHUNK, R)
    v3 = v.reshape(nc, GLA_CHUNK, GLA_V)
    b_last = b3[:, GLA_CHUNK - 1:GLA_CHUNK, :]
    eb = jnp.exp(b)
    q_hat = (q * eb).astype(BF16)
    k_dec = (k3 * jnp.exp(b_last - b3)).reshape(R, R)
    t_in = lax.broadcasted_iota(jnp.int32, (nc, GLA_CHUNK, R), 1)
    ee = ee_ref[...]
    o_intra = jnp.zeros((nc, GLA_CHUNK, GLA_V), F32)
    for s in range(GLA_CHUNK):
        bs = b3[:, s:s + 1, :]
        dec = jnp.exp(jnp.where(t_in >= s, b3 - bs, -jnp.inf))
        p = (q3 * k3[:, s:s + 1, :] * dec).reshape(R, R).astype(BF16)
        o_intra = o_intra + _mm(p, ee).reshape(nc, GLA_CHUNK, GLA_V) * v3[:, s:s + 1, :]
    k_dec_t = k_dec.T
    eb_t = eb.T
    col_chunk = lax.broadcasted_iota(jnp.int32, (R, R), 1) // GLA_CHUNK
    vb = v.astype(BF16)
    bdm = bdm_ref[...]
    state = s_scr[...]
    o_inter = []
    for c in range(nc):
        o_inter.append(_mm(q_hat[c * GLA_CHUNK:(c + 1) * GLA_CHUNK, :], state.astype(BF16)))
        upd = _mm(jnp.where(col_chunk == c, k_dec_t, 0.0).astype(BF16), vb)
        last = (c + 1) * GLA_CHUNK - 1
        state = eb_t[:, last:last + 1] * state + bdm * upd
    s_scr[...] = state
    sfin_ref[...] = state
    o = o_intra.reshape(R, GLA_V) + jnp.concatenate(o_inter, axis=0)
    o_ref[...] = _group_rms(o, hn_ref[...], GLA_DV, gn_ref[...]) * sg_ref[...]


def _gla_seq_call(layer, batch, seq, qkl, v, sg, gn):
    R = LANES
    tps = seq // R
    tri, ee, bdm = _gla_consts()
    hn = _block_ones(GLA_V, GLA_DV)
    tok = lambda w: pl.BlockSpec((R, w), lambda b, t: (b * tps + t, 0))
    return pl.pallas_call(
        _gla_seq_body,
        out_shape=[jax.ShapeDtypeStruct((batch * seq, GLA_V), F32),
                   jax.ShapeDtypeStruct((batch, GLA_QK, GLA_V), F32)],
        grid=(batch, tps),
        in_specs=[tok(3 * R), tok(GLA_V), tok(GLA_V),
                  pl.BlockSpec((None, 1, GLA_V), lambda b, t: (layer, 0, 0)),
                  _const((R, R)), _const((GLA_QK, GLA_V)), _const((GLA_QK, GLA_V)), _const((GLA_V, GLA_V))],
        out_specs=[tok(GLA_V), pl.BlockSpec((None, GLA_QK, GLA_V), lambda b, t: (b, 0, 0))],
        scratch_shapes=[pltpu.VMEM((GLA_QK, GLA_V), F32)],
        compiler_params=_cparams(("arbitrary", "arbitrary")),
        name="gla_seq",
    )(qkl, v, sg, gn, tri, ee, bdm, hn)


def _gla_step_body(qkl_ref, v_ref, sg_ref, gn_ref, s0_ref, exa_ref, exv_ref, red_ref, hn_ref, o_ref, s_ref):
    R = LANES
    q = qkl_ref[:, 0:R]
    k = qkl_ref[:, R:2 * R]
    a_hi, a_lo = _split_bf16(jnp.exp(qkl_ref[:, 2 * R:3 * R]))
    exa = exa_ref[...]
    a_x = _mm(a_hi, exa) + _mm(a_lo, exa)
    k_x = _mm(k.astype(BF16), exa)
    q_x = _mm(q.astype(BF16), exa)
    v_x = _mm(v_ref[...].astype(BF16), exv_ref[...])
    state = a_x * s0_ref[...] + k_x * v_x
    s_ref[...] = state
    o = _mm((q_x * state).astype(BF16), red_ref[...])
    o_ref[...] = _group_rms(o, hn_ref[...], GLA_DV, gn_ref[...]) * sg_ref[...]


def _gla_step_call(layer, qkl, v, sg, gn, s0_flat):
    n = qkl.shape[0]
    tb = min(n, 32)
    sw = GLA_QK * GLA_DV
    idx = jnp.arange(sw)
    h_of, d_of, e_of = idx // (GLA_DK * GLA_DV), (idx // GLA_DV) % GLA_DK, idx % GLA_DV
    exa = (jnp.arange(GLA_QK)[:, None] == (h_of * GLA_DK + d_of)[None, :]).astype(BF16)
    exv = (jnp.arange(GLA_V)[:, None] == (h_of * GLA_DV + e_of)[None, :]).astype(BF16)
    red = exv.T
    hn = _block_ones(GLA_V, GLA_DV)
    tok = lambda w: pl.BlockSpec((tb, w), lambda i: (i, 0))
    return pl.pallas_call(
        _gla_step_body,
        out_shape=[jax.ShapeDtypeStruct((n, GLA_V), F32), jax.ShapeDtypeStruct((n, sw), F32)],
        grid=(n // tb,),
        in_specs=[tok(3 * LANES), tok(GLA_V), tok(GLA_V),
                  pl.BlockSpec((None, 1, GLA_V), lambda i: (layer, 0, 0)),
                  tok(sw), _const((GLA_QK, sw)), _const((GLA_V, sw)), _const((sw, GLA_V)),
                  _const((GLA_V, GLA_V))],
        out_specs=[tok(GLA_V), tok(sw)],
        compiler_params=_cparams(("arbitrary",)),
        name="gla_step",
    )(qkl, v, sg, gn, s0_flat, exa, exv, red, hn)


def _s5_finish(hr, hi, u, cc_ref, d_ref, gw_ref, gb_ref, on_ref, hn_ref):
    y = _mm(hr.astype(BF16), cc_ref[0:S5_S, :]) + _mm(hi.astype(BF16), cc_ref[S5_S:, :]) + d_ref[...] * u
    yg = jax.nn.gelu(y)
    o = yg * jax.nn.sigmoid(_mm(yg.astype(BF16), gw_ref[...]) + gb_ref[...])
    return _group_rms(o, hn_ref[...], S5_J, on_ref[...])


def _s5_seq_body(u_ref, bb_ref, lp_ref, lt_ref, cc_ref, d_ref, gw_ref, gb_ref, on_ref, hn_ref,
                 o_ref, hfin_ref, carry_scr, *, n_rows):
    @pl.when(pl.program_id(1) == 0)
    def _():
        carry_scr[...] = jnp.zeros_like(carry_scr)

    u = u_ref[...]
    x = _mm(u.astype(BF16), bb_ref[...])
    xr, xi = x[:, 0:S5_S], x[:, S5_S:]
    row = lax.broadcasted_iota(jnp.int32, (n_rows, S5_S), 0)
    step, k = 1, 0
    while step < n_rows:
        pr, pi = lp_ref[k, :, 0:S5_S], lp_ref[k, :, S5_S:]
        sr = jnp.where(row >= step, pltpu.roll(xr, step, 0), 0.0)
        si = jnp.where(row >= step, pltpu.roll(xi, step, 0), 0.0)
        xr, xi = xr + (pr * sr - pi * si), xi + (pr * si + pi * sr)
        step, k = step * 2, k + 1
    cr, ci = carry_scr[:, 0:S5_S], carry_scr[:, S5_S:]
    ltr, lti = lt_ref[:, 0:S5_S], lt_ref[:, S5_S:]
    hr = xr + (ltr * cr - lti * ci)
    hi = xi + (ltr * ci + lti * cr)
    carry_scr[:, 0:S5_S] = hr[n_rows - 1:n_rows, :]
    carry_scr[:, S5_S:] = hi[n_rows - 1:n_rows, :]
    hfin_ref[:, 0:S5_S] = hr[n_rows - 1:n_rows, :]
    hfin_ref[:, S5_S:] = hi[n_rows - 1:n_rows, :]
    o_ref[...] = _s5_finish(hr, hi, u, cc_ref, d_ref, gw_ref, gb_ref, on_ref, hn_ref)


def _s5_step_body(u_ref, h0r_ref, h0i_ref, bb_ref, lam_ref, cc_ref, d_ref, gw_ref, gb_ref, on_ref, hn_ref,
                  o_ref, hr_ref, hi_ref):
    u = u_ref[...]
    x = _mm(u.astype(BF16), bb_ref[...])
    lr, li = lam_ref[:, 0:S5_S], lam_ref[:, S5_S:]
    h0r, h0i = h0r_ref[...], h0i_ref[...]
    hr = x[:, 0:S5_S] + (lr * h0r - li * h0i)
    hi = x[:, S5_S:] + (lr * h0i + li * h0r)
    hr_ref[...] = hr
    hi_ref[...] = hi
    o_ref[...] = _s5_finish(hr, hi, u, cc_ref, d_ref, gw_ref, gb_ref, on_ref, hn_ref)


def _s5_tail_specs(layer, idx):
    lw = lambda shape: pl.BlockSpec((None,) + shape, lambda *a: (layer,) + (0,) * len(shape))
    return [lw((2 * S5_S, S5_W)), lw((1, S5_W)), lw((S5_W, S5_W)), lw((1, S5_W)), lw((1, S5_W)),
            _const((S5_W, S5_W))]


def _s5_seq_call(layer, batch, seq, u, pp):
    R = pp["s5_lt"].shape[1]
    tps = seq // R
    nsteps = pp["s5_lp"].shape[1]
    hn = _block_ones(S5_W, S5_J)
    lw = lambda shape: pl.BlockSpec((None,) + shape, lambda b, t: (layer,) + (0,) * len(shape))
    tok = pl.BlockSpec((R, S5_W), lambda b, t: (b * tps + t, 0))
    return pl.pallas_call(
        functools.partial(_s5_seq_body, n_rows=R),
        out_shape=[jax.ShapeDtypeStruct((batch * seq, S5_W), F32),
                   jax.ShapeDtypeStruct((batch, 1, 2 * S5_S), F32)],
        grid=(batch, tps),
        in_specs=[tok, lw((S5_W, 2 * S5_S)), lw((nsteps, 1, 2 * S5_S)), lw((R, 2 * S5_S))]
        + _s5_tail_specs(layer, None),
        out_specs=[tok, pl.BlockSpec((None, 1, 2 * S5_S), lambda b, t: (b, 0, 0))],
        scratch_shapes=[pltpu.VMEM((1, 2 * S5_S), F32)],
        compiler_params=_cparams(("arbitrary", "arbitrary")),
        name="s5_seq",
    )(u, pp["s5_bb"], pp["s5_lp"], pp["s5_lt"], pp["s5_cc"], pp["s5_d"], pp["s5_glu_w"], pp["s5_glu_b"],
      pp["s5_o_norm"], hn)


def _s5_step_call(layer, u, h0r, h0i, pp):
    n = u.shape[0]
    hn = _block_ones(S5_W, S5_J)
    lw = lambda shape: pl.BlockSpec((None,) + shape, lambda i: (layer,) + (0,) * len(shape))
    st = pl.BlockSpec((n, S5_S), lambda i: (0, 0))
    tok = pl.BlockSpec((n, S5_W), lambda i: (0, 0))
    return pl.pallas_call(
        _s5_step_body,
        out_shape=[jax.ShapeDtypeStruct((n, S5_W), F32), jax.ShapeDtypeStruct((n, S5_S), F32),
                   jax.ShapeDtypeStruct((n, S5_S), F32)],
        grid=(1,),
        in_specs=[tok, st, st, lw((S5_W, 2 * S5_S)), lw((1, 2 * S5_S))] + _s5_tail_specs(layer, None),
        out_specs=[tok, st, st],
        compiler_params=_cparams(("arbitrary",)),
        name="s5_step",
    )(u, h0r, h0i, pp["s5_bb"], pp["s5_lam"], pp["s5_cc"], pp["s5_d"], pp["s5_glu_w"], pp["s5_glu_b"],
      pp["s5_o_norm"], hn)


def _mla_out(ctx_pairs, wv_ref, mn_ref, hn_ref):
    outs = [_mm(cp.astype(BF16), wv_ref[p]) for p, cp in enumerate(ctx_pairs)]
    o = jnp.concatenate(outs, axis=1)
    return _group_rms(o, hn_ref[...], MLA_DV, mn_ref[...])


def _flash_body(q_ref, k_ref, wv_ref, mn_ref, hn_ref, o_ref, m_scr, l_scr, acc_scr, *, tq):
    qi = pl.program_id(1)
    ki = pl.program_id(2)
    L = LANES
    reps = tq // L

    @pl.when(ki == 0)
    def _():
        m_scr[...] = jnp.full_like(m_scr, -jnp.inf)
        l_scr[...] = jnp.zeros_like(l_scr)
        acc_scr[...] = jnp.zeros_like(acc_scr)

    def step(masked):
        kc = k_ref[...]
        vv = kc[:, 0:L]
        if masked:
            keep = (lax.broadcasted_iota(jnp.int32, (tq, tq), 1) <= lax.broadcasted_iota(jnp.int32, (tq, tq), 0))
        for h in range(MLA_H):
            s = _mm_nt(q_ref[:, 2 * h * L:(2 * h + 2) * L], kc)
            if masked:
                s = jnp.where(keep, s, -jnp.inf)
            m_prev = m_scr[h]
            m_new = jnp.maximum(m_prev, jnp.max(s, axis=1, keepdims=True))
            p = jnp.exp(s - jnp.concatenate([m_new] * reps, axis=1))
            alpha = jnp.exp(m_prev - m_new)
            l_scr[h] = alpha * l_scr[h] + jnp.sum(p, axis=1, keepdims=True)
            acc_scr[h] = alpha * acc_scr[h] + _mm(p.astype(BF16), vv)
            m_scr[h] = m_new

    @pl.when(ki < qi)
    def _():
        step(False)

    @pl.when(ki == qi)
    def _():
        step(True)
        pairs = [jnp.concatenate([acc_scr[2 * p] / l_scr[2 * p], acc_scr[2 * p + 1] / l_scr[2 * p + 1]], axis=1)
                 for p in range(MLA_H // 2)]
        o_ref[...] = _mla_out(pairs, wv_ref, mn_ref, hn_ref)


def _flash_call(layer, batch, seq, qcat, kcat, pp):
    tq = min(512, seq)
    nq = seq // tq
    L = LANES
    hn = _block_ones(MLA_O, MLA_DV)
    lw = lambda shape: pl.BlockSpec((None,) + shape, lambda b, i, j: (layer,) + (0,) * len(shape))
    return pl.pallas_call(
        functools.partial(_flash_body, tq=tq),
        out_shape=jax.ShapeDtypeStruct((batch * seq, MLA_O), F32),
        grid=(batch, nq, nq),
        in_specs=[pl.BlockSpec((tq, MLA_H * 2 * L), lambda b, i, j: (b * nq + i, 0)),
                  pl.BlockSpec((tq, 2 * L), lambda b, i, j: (b * nq + jnp.minimum(i, j), 0)),
                  lw((MLA_H // 2, 2 * L, L)), lw((1, MLA_O)), _const((MLA_O, MLA_O))],
        out_specs=pl.BlockSpec((tq, MLA_O), lambda b, i, j: (b * nq + i, 0)),
        scratch_shapes=[pltpu.VMEM((MLA_H, tq, L), F32)] * 3,
        compiler_params=_cparams(("arbitrary", "arbitrary", "arbitrary")),
        name="mla_flash",
    )(qcat, kcat, pp["w_v"], pp["mla_o_norm"], hn)


def _decode_body(pt_ref, q_ref, kc_ref, lat_hbm, rope_hbm, o_ref, lat_buf, rope_buf, sem,
                 *, layer, n_pages, page):
    b = pl.program_id(0)
    nb = pl.num_programs(0)
    slot = b % 2
    L = LANES

    def copies(bb, sl):
        out = []
        for j in range(n_pages):
            pg = pt_ref[bb * n_pages + j]
            out.append(pltpu.make_async_copy(lat_hbm.at[layer, pg], lat_buf.at[sl, pl.ds(j * page, page)],
                                             sem.at[0, sl]))
            out.append(pltpu.make_async_copy(rope_hbm.at[layer, pg], rope_buf.at[sl, pl.ds(j * page, page)],
                                             sem.at[1, sl]))
        return out

    @pl.when(b == 0)
    def _():
        for c in copies(0, 0):
            c.start()

    @pl.when(b + 1 < nb)
    def _():
        for c in copies(b + 1, 1 - slot):
            c.start()

    for c in copies(b, slot):
        c.wait()

    q = q_ref[...].astype(BF16)
    lat = lat_buf[slot].astype(BF16)
    rope = rope_buf[slot].astype(BF16)
    s = _mm_nt(q[:, 0:L], lat) + _mm_nt(q[:, L:L + MLA_ROPE], rope)
    kc = kc_ref[...].astype(BF16).astype(F32)
    s_cur = jnp.sum(q.astype(F32) * kc, axis=1, keepdims=True)
    m = jnp.maximum(jnp.max(s, axis=1, keepdims=True), s_cur)
    p = jnp.exp(s - m)
    p_cur = jnp.exp(s_cur - m)
    den = jnp.sum(p, axis=1, keepdims=True) + p_cur
    ctx = _mm(p.astype(BF16), lat) + p_cur.astype(BF16).astype(F32) * kc[:, 0:L]
    o_ref[...] = ctx / den


def _decode_call(layer, page_table, q3, kcat3, cache_lat, cache_rope):
    n, n_pages = page_table.shape
    page = cache_lat.shape[2]
    past = n_pages * page
    L = LANES
    grid_spec = pltpu.PrefetchScalarGridSpec(
        num_scalar_prefetch=1,
        grid=(n,),
        in_specs=[pl.BlockSpec((None, MLA_H, 2 * L), lambda b, pt: (b, 0, 0)),
                  pl.BlockSpec((None, 1, 2 * L), lambda b, pt: (b, 0, 0)),
                  pl.BlockSpec(memory_space=pl.ANY),
                  pl.BlockSpec(memory_space=pl.ANY)],
        out_specs=pl.BlockSpec((None, MLA_H, L), lambda b, pt: (b, 0, 0)),
        scratch_shapes=[pltpu.VMEM((2, past, MLA_R), F32),
                        pltpu.VMEM((2, past, MLA_ROPE), F32),
                        pltpu.SemaphoreType.DMA((2, 2))],
    )
    return pl.pallas_call(
        functools.partial(_decode_body, layer=layer, n_pages=n_pages, page=page),
        out_shape=jax.ShapeDtypeStruct((n, MLA_H, L), F32),
        grid_spec=grid_spec,
        compiler_params=_cparams(("arbitrary",)),
        name="mla_decode",
    )(page_table.reshape(-1), q3, kcat3, cache_lat, cache_rope)


def _mla_post_body(ctx_ref, wv_ref, mn_ref, hn_ref, o_ref):
    L = LANES
    pairs = [ctx_ref[:, 2 * p * L:(2 * p + 2) * L] for p in range(MLA_H // 2)]
    o_ref[...] = _mla_out(pairs, wv_ref, mn_ref, hn_ref)


def _mla_post_call(layer, ctx, pp):
    n = ctx.shape[0]
    L = LANES
    hn = _block_ones(MLA_O, MLA_DV)
    lw = lambda shape: pl.BlockSpec((None,) + shape, lambda i: (layer,) + (0,) * len(shape))
    return pl.pallas_call(
        _mla_post_body,
        out_shape=jax.ShapeDtypeStruct((n, MLA_O), F32),
        grid=(1,),
        in_specs=[pl.BlockSpec((n, MLA_H * L), lambda i: (0, 0)), lw((MLA_H // 2, 2 * L, L)), lw((1, MLA_O)),
                  _const((MLA_O, MLA_O))],
        out_specs=pl.BlockSpec((n, MLA_O), lambda i: (0, 0)),
        compiler_params=_cparams(("arbitrary",)),
        name="mla_post",
    )(ctx, pp["w_v"], pp["mla_o_norm"], hn)


def _prep_params(w_in, gla_w_a2, gla_b_a, gla_o_norm, s5_a_re, s5_a_im, s5_log_dt, s5_b_re, s5_b_im, s5_c_re,
                 s5_c_im, s5_d, s5_glu_w, s5_glu_b, s5_o_norm, mla_q_norm, mla_kv_norm, mla_w_uq, mla_w_uk,
                 mla_w_uv, mla_o_norm, s5_rows):
    depth, d, _ = w_in.shape
    L = LANES
    pp = {}
    o = np.cumsum([0, GLA_QK, GLA_QK, GLA_V, GLA_RANK, GLA_V, S5_W, MLA_QR, MLA_R, MLA_ROPE])
    gq, gk, gv, ga, gg, su, mcq, mckv, mkr = [w_in[:, :, o[i]:o[i + 1]] for i in range(9)]
    half = MLA_ROPE // 2
    zeros = lambda w: jnp.zeros((depth, d, w), w_in.dtype)
    g11 = jnp.concatenate([mkr, ga, zeros(L - MLA_ROPE - GLA_RANK)], axis=-1)
    g12 = jnp.concatenate([-mkr[..., half:], mkr[..., :half], zeros(L - MLA_ROPE)], axis=-1)
    pp["w_in"] = jnp.concatenate([gq, gk, gv, gg, su, mcq, mckv, g11, g12], axis=-1).astype(BF16)
    wa = jnp.zeros((depth, L, GLA_QK), F32).at[:, MLA_ROPE:MLA_ROPE + GLA_RANK, :].set(gla_w_a2)
    pp["w_a2"] = wa.astype(BF16)
    pp["b_a"] = gla_b_a.reshape(depth, 1, GLA_QK)
    pp["gla_o_norm"] = gla_o_norm.reshape(depth, 1, GLA_V)
    uq = mla_w_uq.reshape(depth, MLA_QR, MLA_H, MLA_NOPE + MLA_ROPE)
    pp["w_n"] = uq[..., :MLA_NOPE].reshape(depth, MLA_QR, MLA_H * MLA_NOPE).astype(BF16)
    rp = uq[..., MLA_NOPE:]
    pad = jnp.zeros((depth, MLA_QR, MLA_H, L - MLA_ROPE), F32)
    pp["w_r"] = jnp.concatenate([rp, pad], axis=-1).reshape(depth, MLA_QR, MLA_H * L).astype(BF16)
    pp["w_rr"] = jnp.concatenate([-rp[..., half:], rp[..., :half], pad], axis=-1).reshape(
        depth, MLA_QR, MLA_H * L).astype(BF16)
    wk_t = jnp.transpose(mla_w_uk, (0, 2, 3, 1))
    wk = jnp.zeros((depth, MLA_H // 2, 2 * MLA_NOPE, 2 * MLA_R), F32)
    wk = wk.at[:, :, :MLA_NOPE, :MLA_R].set(wk_t[:, 0::2]).at[:, :, MLA_NOPE:, MLA_R:].set(wk_t[:, 1::2])
    pp["w_k"] = wk.astype(BF16)
    wv_h = jnp.transpose(mla_w_uv, (0, 2, 1, 3))
    wv = jnp.zeros((depth, MLA_H // 2, 2 * MLA_R, 2 * MLA_DV), F32)
    wv = wv.at[:, :, :MLA_R, :MLA_DV].set(wv_h[:, 0::2]).at[:, :, MLA_R:, MLA_DV:].set(wv_h[:, 1::2])
    pp["w_v"] = wv.astype(BF16)
    pp["q_norm"] = mla_q_norm.reshape(depth, 1, MLA_QR)
    pp["kv_norm"] = mla_kv_norm.reshape(depth, 1, MLA_R)
    pp["mla_o_norm"] = mla_o_norm.reshape(depth, 1, MLA_O)
    dt = jnp.exp(s5_log_dt)[:, :, None]
    mag = jnp.exp(s5_a_re * dt)
    lr, li = mag * jnp.cos(s5_a_im * dt), mag * jnp.sin(s5_a_im * dt)
    den = s5_a_re * s5_a_re + s5_a_im * s5_a_im
    zr = ((lr - 1.0) * s5_a_re + li * s5_a_im) / den
    zi = (li * s5_a_re - (lr - 1.0) * s5_a_im) / den
    bb_re = zr[..., None] * s5_b_re - zi[..., None] * s5_b_im
    bb_im = zr[..., None] * s5_b_im + zi[..., None] * s5_b_re
    eye = jnp.eye(S5_G, dtype=F32)
    bd_in = lambda t: jnp.einsum("lgnj,gh->lgjhn", t, eye).reshape(depth, S5_W, S5_S)
    pp["s5_bb"] = jnp.concatenate([bd_in(bb_re), bd_in(bb_im)], axis=-1).astype(BF16)
    bd_out = lambda t: jnp.einsum("lgjn,gh->lgnhj", t, eye).reshape(depth, S5_S, S5_W)
    pp["s5_cc"] = jnp.concatenate([bd_out(s5_c_re), -bd_out(s5_c_im)], axis=1).astype(BF16)
    pp["s5_lam"] = jnp.concatenate([lr.reshape(depth, 1, S5_S), li.reshape(depth, 1, S5_S)], axis=-1)

    def lam_pow(m):
        m = jnp.asarray(m, F32)[None, :, None, None]
        mg = jnp.exp(m * (s5_a_re * dt)[:, None])
        ang = m * (s5_a_im * dt)[:, None]
        n = mg.shape[1]
        return jnp.concatenate([(mg * jnp.cos(ang)).reshape(depth, n, S5_S),
                                (mg * jnp.sin(ang)).reshape(depth, n, S5_S)], axis=-1)

    nsteps = max(1, int(math.log2(s5_rows)))
    pp["s5_lp"] = lam_pow([2 ** k for k in range(nsteps)])[:, :, None, :]
    pp["s5_lt"] = lam_pow(np.arange(1, s5_rows + 1))
    pp["s5_d"] = s5_d.reshape(depth, 1, S5_W)
    pp["s5_glu_w"] = s5_glu_w.astype(BF16)
    pp["s5_glu_b"] = s5_glu_b.reshape(depth, 1, S5_W)
    pp["s5_o_norm"] = s5_o_norm.reshape(depth, 1, S5_W)
    return pp


def _rope_tables(pos):
    half = MLA_ROPE // 2
    freqs = ROPE_THETA ** (-jnp.arange(half, dtype=F32) / half)
    ang = pos.astype(F32)[:, None] * freqs[None, :]
    pad = jnp.zeros((pos.shape[0], LANES - MLA_ROPE), F32)
    cos, sin = jnp.cos(ang), jnp.sin(ang)
    return jnp.concatenate([cos, cos, pad], axis=1), jnp.concatenate([sin, sin, pad], axis=1)


def kernel(x_prompt, x_sample, state_gla, state_s5_re, state_s5_im, cache_kv_latent, cache_k_rope, page_table, c_prompt, c_sample, ada_w, ada_b, norm_ffn1, ffn1_w_gate, ffn1_w_up, ffn1_w_down, norm_mix, w_in, gla_w_a2, gla_b_a, gla_o_norm, s5_a_re, s5_a_im, s5_log_dt, s5_b_re, s5_b_im, s5_c_re, s5_c_im, s5_d, s5_glu_w, s5_glu_b, s5_o_norm, mla_q_norm, mla_kv_norm, mla_w_uq, mla_w_uk, mla_w_uv, mla_o_norm, w_out, norm_ffn2, ffn2_w_gate, ffn2_w_up, ffn2_w_down, final_norm):
    bp, seq, d = x_prompt.shape
    bs = x_sample.shape[0]
    depth = ada_w.shape[0]
    past = page_table.shape[1] * cache_kv_latent.shape[2]
    assert x_sample.shape[1] == 1 and seq % LANES == 0 and bs % 8 == 0
    s5_rows = LANES

    pp = _prep_params(w_in, gla_w_a2, gla_b_a, gla_o_norm, s5_a_re, s5_a_im, s5_log_dt, s5_b_re, s5_b_im,
                      s5_c_re, s5_c_im, s5_d, s5_glu_w, s5_glu_b, s5_o_norm, mla_q_norm, mla_kv_norm,
                      mla_w_uq, mla_w_uk, mla_w_uv, mla_o_norm, s5_rows)
    bf = lambda w: w.astype(BF16)
    f1 = (bf(ffn1_w_gate), bf(ffn1_w_up), bf(ffn1_w_down))
    f2 = (bf(ffn2_w_gate), bf(ffn2_w_up), bf(ffn2_w_down))
    wo = bf(w_out)
    n1 = norm_ffn1.reshape(depth, 1, d)
    nm = norm_mix.reshape(depth, 1, d)
    n2 = norm_ffn2.reshape(depth, 1, d)
    fn = final_norm.reshape(1, d)

    pad_rows = (-(bs + bp)) % 8
    c_all = jnp.concatenate([c_sample, c_prompt, jnp.zeros((pad_rows, d), F32)], axis=0)
    mod = _ada_call(c_all, ada_w, ada_b)
    mod_p = mod[:, bs:bs + bp].reshape(depth, bp, 1, N_MOD * d)

    cos_p, sin_p = _rope_tables(jnp.arange(seq))
    cos_s, sin_s = _rope_tables(jnp.full((bs,), past))

    tm = min(512, seq)
    xp = x_prompt.reshape(bp * seq, d)
    xs = x_sample.reshape(bs, d)
    outs = {k: [] for k in ("gla_p", "gla_s", "s5r_p", "s5i_p", "s5r_s", "s5i_s", "lat_p", "rope_p", "lat_s",
                            "rope_s")}
    for l in range(depth):
        rp = _Rows(bp * seq, tm, l, mod_p, False, seq // tm)
        rs = _Rows(bs, bs, l, mod, True, 1)
        last = l == depth - 1
        xp = _ffn_call(rp, xp, 0, n1, *f1)
        qkl, v, sg, su, qcat, kcat, ckv, kr = _inproj_call(rp, xp, nm, pp, cos_p, sin_p, BF16)
        o_gla, s_gla = _gla_seq_call(l, bp, seq, qkl, v, sg, pp["gla_o_norm"])
        o_s5, h_fin = _s5_seq_call(l, bp, seq, su, pp)
        o_mla = _flash_call(l, bp, seq, qcat, kcat, pp)
        xp = _ffn_call(rp, xp, 6, n2, *f2, mix=(o_gla, o_s5, o_mla, wo), final_norm=fn if last else None)
        outs["gla_p"].append(jnp.stack([s_gla[:, h * GLA_DK:(h + 1) * GLA_DK, h * GLA_DV:(h + 1) * GLA_DV]
                                        for h in range(GLA_H)], axis=1))
        outs["s5r_p"].append(h_fin[:, 0, :S5_S].reshape(bp, S5_G, S5_N))
        outs["s5i_p"].append(h_fin[:, 0, S5_S:].reshape(bp, S5_G, S5_N))
        outs["lat_p"].append(ckv.reshape(bp, seq, MLA_R))
        outs["rope_p"].append(kr[:, :MLA_ROPE].reshape(bp, seq, MLA_ROPE))
        xs = _ffn_call(rs, xs, 0, n1, *f1)
        qkl, v, sg, su, qcat, kcat, ckv, kr = _inproj_call(rs, xs, nm, pp, cos_s, sin_s, F32)
        o_gla, s_new = _gla_step_call(l, qkl, v, sg, pp["gla_o_norm"], state_gla[l].reshape(bs, -1))
        o_s5, hr, hi = _s5_step_call(l, su, state_s5_re[l].reshape(bs, S5_S), state_s5_im[l].reshape(bs, S5_S), pp)
        ctx = _decode_call(l, page_table, qcat.reshape(bs, MLA_H, 2 * LANES), kcat.reshape(bs, 1, 2 * LANES),
                           cache_kv_latent, cache_k_rope)
        o_mla = _mla_post_call(l, ctx.reshape(bs, MLA_H * LANES), pp)
        xs = _ffn_call(rs, xs, 6, n2, *f2, mix=(o_gla, o_s5, o_mla, wo), final_norm=fn if last else None)
        outs["gla_s"].append(s_new.reshape(bs, GLA_H, GLA_DK, GLA_DV))
        outs["s5r_s"].append(hr.reshape(bs, S5_G, S5_N))
        outs["s5i_s"].append(hi.reshape(bs, S5_G, S5_N))
        outs["lat_s"].append(ckv.reshape(bs, 1, MLA_R))
        outs["rope_s"].append(kr[:, :MLA_ROPE].reshape(bs, 1, MLA_ROPE))

    st = lambda k: jnp.stack(outs[k], axis=0)
    return (xp.reshape(bp, seq, d), xs.reshape(bs, 1, d), st("gla_p"), st("gla_s"), st("s5r_p"), st("s5i_p"),
            st("s5r_s"), st("s5i_s"), st("lat_p"), st("rope_p"), st("lat_s"), st("rope_s"))
```
